```python
import jax, jax.numpy as jnp
from jax import lax
import numpy as np

D_MODEL = 2048
BATCH = 1
SEQ = 16384
DEPTH = 2
DEC_BATCH = 32
DEC_SEQ = 32
PAST_LEN = 4096

CHUNK = 64
GROUP_W = D_MODEL // 16
POOL_WINDOWS = (2, 4, 8, 16)
POOL_GROUPS = len(POOL_WINDOWS)
POOL_W = POOL_GROUPS * GROUP_W
POOL_HIST = max(POOL_WINDOWS) - 1
SGU_HEADS = 6
SGU_W = SGU_HEADS * GROUP_W
SGU_CHUNK = 128
CONV_GROUPS = 6
CONV_W = CONV_GROUPS * GROUP_W
CONV_K = 31
CONV_HIST = CONV_K - 1
MIX_W = POOL_W + SGU_W + CONV_W
IN_W = POOL_W + 2 * SGU_W + 2 * CONV_W
D_FF = ((8 * D_MODEL + 3 * 256 - 1) // (3 * 256)) * 256
RMS_EPS = 1e-6
LN_EPS = 1e-5

kernel_name = "hybrid_pool_sgu_conformer_stream_step"


def rms_norm(x, g):
    xf = x.astype(jnp.float32)
    y = xf * lax.rsqrt(jnp.mean(xf * xf, axis=-1, keepdims=True) + RMS_EPS)
    return (y * g.astype(jnp.float32)).astype(x.dtype)


def layer_norm(x, g, b):
    xf = x.astype(jnp.float32)
    mu = jnp.mean(xf, axis=-1, keepdims=True)
    xc = xf - mu
    y = xc * lax.rsqrt(jnp.mean(xc * xc, axis=-1, keepdims=True) + LN_EPS)
    return (y * g.astype(jnp.float32) + b.astype(jnp.float32)).astype(x.dtype)


def pool_mixer(xa, hist, pos0, pool_w, pool_scale):
    B, S, _ = xa.shape
    ext = jnp.concatenate([hist, xa], axis=1)
    cs = jnp.cumsum(ext.astype(jnp.float32), axis=1)
    cs = jnp.pad(cs, ((0, 0), (1, 0), (0, 0)))
    pos = pos0 + jnp.arange(S)
    means = []
    for g, w in enumerate(POOL_WINDOWS):
        sl = slice(g * GROUP_W, (g + 1) * GROUP_W)
        s = cs[:, POOL_HIST + 1:POOL_HIST + 1 + S, sl] - cs[:, POOL_HIST + 1 - w:POOL_HIST + 1 - w + S, sl]
        cnt = jnp.minimum(pos + 1, w).astype(jnp.float32)[None, :, None]
        means.append(s / cnt)
    pooled = jnp.concatenate(means, axis=-1).astype(xa.dtype) - xa
    pg = pooled.reshape(B, S, POOL_GROUPS, GROUP_W)
    out = jnp.einsum('bsgc,gcd->bsgd', pg, pool_w).reshape(B, S, POOL_W)
    return out * pool_scale, ext[:, -POOL_HIST:]


def sgu_mixer(u, v, norm_g, ws, bias):
    B, S, _ = u.shape
    v = rms_norm(v, norm_g)
    pad = (-S) % SGU_CHUNK
    vp = jnp.pad(v, ((0, 0), (0, pad), (0, 0)))
    n = (S + pad) // SGU_CHUNK
    vc = vp.reshape(B, n, SGU_CHUNK, SGU_HEADS, GROUP_W)
    idx = jnp.arange(SGU_CHUNK)
    mask = (idx[None, :] // CHUNK) <= (idx[:, None] // CHUNK)
    wm = jnp.where(mask[None], ws, jnp.zeros_like(ws))
    mixed = jnp.einsum('hij,bnjhc->bnihc', wm, vc) + bias.T[None, None, :, :, None]
    mixed = mixed.reshape(B, n * SGU_CHUNK, SGU_W)[:, :S]
    return u * mixed, v


def conv_mixer(a, gate, hist, conv_w, conv_b, ln_g, ln_b):
    h = a * jax.nn.sigmoid(gate)
    ext = jnp.concatenate([hist, h], axis=1)
    y = lax.conv_general_dilated(ext, conv_w[:, None, :].astype(ext.dtype), window_strides=(1,),
                                 padding='VALID', dimension_numbers=('NWC', 'WIO', 'NWC'),
                                 feature_group_count=CONV_W) + conv_b
    y = jax.nn.silu(layer_norm(y, ln_g, ln_b))
    return y, ext[:, -CONV_HIST:]


def trunk_layer(x, pool_hist, conv_hist, pos0, g_pre_mix, g_post_mix, g_pre_ffn, g_post_ffn,
                w_in, w_out, pool_w, pool_scale, sgu_norm, sgu_ws, sgu_b,
                conv_w, conv_b, conv_ln_g, conv_ln_b, w_gate, w_up, w_down):
    h = rms_norm(x, g_pre_mix)
    z = h @ w_in
    xa, u, v, ca, cg = jnp.split(z, [POOL_W, POOL_W + SGU_W, POOL_W + 2 * SGU_W,
                                     POOL_W + 2 * SGU_W + CONV_W], axis=-1)
    ya, new_pool = pool_mixer(xa, pool_hist, pos0, pool_w, pool_scale)
    yb, v_rows = sgu_mixer(u, v, sgu_norm, sgu_ws, sgu_b)
    yc, new_conv = conv_mixer(ca, cg, conv_hist, conv_w, conv_b, conv_ln_g, conv_ln_b)
    mix = jnp.concatenate([ya, yb, yc], axis=-1) @ w_out
    x = x + rms_norm(mix, g_post_mix)
    f = rms_norm(x, g_pre_ffn)
    f = (jax.nn.silu(f @ w_gate) * (f @ w_up)) @ w_down
    x = x + rms_norm(f, g_post_ffn)
    return x, new_pool, new_conv, v_rows


def setup_inputs(seed: int = 0) -> dict:
    key = jax.random.key(seed)
    ks = jax.random.split(key, 24)
    f32 = jnp.float32
    nrm = lambda k, shape, s: jax.random.normal(k, shape, f32) * s
    gain = lambda k, shape: 1.0 + 0.05 * jax.random.normal(k, shape, f32)
    return {
        "x_prompt": jax.random.normal(ks[0], (BATCH, SEQ, D_MODEL), f32),
        "x_sample": jax.random.normal(ks[1], (DEC_BATCH, DEC_SEQ, D_MODEL), f32),
        "state_pool": jax.random.normal(ks[2], (DEPTH, DEC_BATCH, POOL_HIST, POOL_W), f32),
        "state_conv": jax.random.normal(ks[3], (DEPTH, DEC_BATCH, CONV_HIST, CONV_W), f32),
        "norm_pre_mix": gain(ks[4], (DEPTH, D_MODEL)),
        "norm_post_mix": gain(ks[5], (DEPTH, D_MODEL)),
        "norm_pre_ffn": gain(ks[6], (DEPTH, D_MODEL)),
        "norm_post_ffn": gain(ks[7], (DEPTH, D_MODEL)),
        "w_in": nrm(ks[8], (DEPTH, D_MODEL, IN_W), D_MODEL ** -0.5),
        "w_out": nrm(ks[9], (DEPTH, MIX_W, D_MODEL), MIX_W ** -0.5),
        "pool_w": nrm(ks[10], (DEPTH, POOL_GROUPS, GROUP_W, GROUP_W), GROUP_W ** -0.5),
        "pool_scale": gain(ks[11], (DEPTH, POOL_W)),
        "sgu_norm": gain(ks[12], (DEPTH, SGU_W)),
        "sgu_ws": nrm(ks[13], (DEPTH, SGU_HEADS, SGU_CHUNK, SGU_CHUNK), SGU_CHUNK ** -0.5),
        "sgu_b": 1.0 + nrm(ks[14], (DEPTH, SGU_HEADS, SGU_CHUNK), 0.01),
        "conv_w": nrm(ks[15], (DEPTH, CONV_K, CONV_W), CONV_K ** -0.5),
        "conv_b": nrm(ks[16], (DEPTH, CONV_W), 0.01),
        "conv_ln_g": gain(ks[17], (DEPTH, CONV_W)),
        "conv_ln_b": nrm(ks[18], (DEPTH, CONV_W), 0.01),
        "w_gate": nrm(ks[19], (DEPTH, D_MODEL, D_FF), D_MODEL ** -0.5),
        "w_up": nrm(ks[20], (DEPTH, D_MODEL, D_FF), D_MODEL ** -0.5),
        "w_down": nrm(ks[21], (DEPTH, D_FF, D_MODEL), D_FF ** -0.5),
    }


def reference(x_prompt, x_sample, state_pool, state_conv, norm_pre_mix, norm_post_mix, norm_pre_ffn,
              norm_post_ffn, w_in, w_out, pool_w, pool_scale, sgu_norm, sgu_ws, sgu_b,
              conv_w, conv_b, conv_ln_g, conv_ln_b, w_gate, w_up, w_down):
    bp = x_prompt.shape[0]
    hp, hs = x_prompt, x_sample
    pool_p, conv_p, pool_s, conv_s, vrows_s = [], [], [], [], []
    for l in range(DEPTH):
        params = (norm_pre_mix[l], norm_post_mix[l], norm_pre_ffn[l], norm_post_ffn[l],
                  w_in[l], w_out[l], pool_w[l], pool_scale[l], sgu_norm[l], sgu_ws[l], sgu_b[l],
                  conv_w[l], conv_b[l], conv_ln_g[l], conv_ln_b[l], w_gate[l], w_up[l], w_down[l])
        zp = jnp.zeros((bp, POOL_HIST, POOL_W), hp.dtype)
        zc = jnp.zeros((bp, CONV_HIST, CONV_W), hp.dtype)
        hp, npool, nconv, _ = trunk_layer(hp, zp, zc, 0, *params)
        pool_p.append(npool)
        conv_p.append(nconv)
        hs, npool_s, nconv_s, v_s = trunk_layer(hs, state_pool[l].astype(hs.dtype),
                                                state_conv[l].astype(hs.dtype), PAST_LEN, *params)
        pool_s.append(npool_s)
        conv_s.append(nconv_s)
        vrows_s.append(v_s)
    return (hp, hs, jnp.stack(pool_p), jnp.stack(conv_p), jnp.stack(pool_s), jnp.stack(conv_s), jnp.stack(vrows_s))
```

```python
import functools

import jax
import jax.numpy as jnp
from jax import lax
from jax.experimental import pallas as pl
from jax.experimental.pallas import tpu as pltpu

F32 = jnp.float32
BF16 = jnp.bfloat16

GROUP_W = 128
POOL_WINDOWS = (2, 4, 8, 16)
POOL_W = len(POOL_WINDOWS) * GROUP_W
POOL_HIST = max(POOL_WINDOWS) - 1
POOL_PAD = 16
SGU_HEADS = 6
SGU_W = SGU_HEADS * GROUP_W
SGU_CHUNK = 128
CAUSAL_CHUNK = 64
CONV_W = 6 * GROUP_W
CONV_K = 31
CONV_HIST = CONV_K - 1
CONV_PAD = 32
PAST_LEN = 4096
RMS_EPS = 1e-6
LN_EPS = 1e-5

ROW_CHUNK = 128
TM_TARGET = 512
TS_TARGET = 512
TN_IN = 512
TF_FFN = 512
VMEM_LIMIT_BYTES = 56 * 1024 * 1024


def _rms(x, g):
    ms = jnp.mean(x * x, axis=-1, keepdims=True)
    return x * lax.rsqrt(ms + RMS_EPS) * g


def _sigmoid(x):
    return 1.0 / (1.0 + jnp.exp(-x))


def _row_loop(n_rows, chunk, body):
    def step(c, carry):
        body(pl.ds(pl.multiple_of(c * chunk, chunk), chunk))
        return carry
    lax.fori_loop(0, n_rows // chunk, step, 0)


def _params(sem):
    return pltpu.CompilerParams(dimension_semantics=sem, vmem_limit_bytes=VMEM_LIMIT_BYTES)


def _in_proj_kernel(x_ref, g_ref, w_ref, z_ref, hn_ref, *, tm, rc):
    @pl.when(pl.program_id(1) == 0)
    def _():
        def body(r):
            hn_ref[r, :] = _rms(x_ref[r, :], g_ref[...]).astype(BF16)
        _row_loop(tm, rc, body)

    z_ref[...] = jnp.dot(hn_ref[...], w_ref[...], preferred_element_type=F32)


def _in_proj(x, g, w, layer, *, tm, tn):
    m, d = x.shape
    n = w.shape[-1]
    rc = min(ROW_CHUNK, tm)
    return pl.pallas_call(
        functools.partial(_in_proj_kernel, tm=tm, rc=rc),
        grid=(m // tm, n // tn),
        in_specs=[
            pl.BlockSpec((tm, d), lambda i, j: (i, 0)),
            pl.BlockSpec((None, 1, d), lambda i, j: (layer, 0, 0)),
            pl.BlockSpec((None, d, tn), lambda i, j: (layer, 0, j)),
        ],
        out_specs=pl.BlockSpec((tm, tn), lambda i, j: (i, j)),
        out_shape=jax.ShapeDtypeStruct((m, n), F32),
        scratch_shapes=[pltpu.VMEM((tm, d), BF16)],
        compiler_params=_params(("parallel", "arbitrary")),
        name="in_proj",
    )(x, g, w)


def _mixer_kernel(z_ref, ph_ref, ch_ref, pw_ref, ps_ref, sn_ref, ws_ref, sbt_ref,
                  cw_ref, cb_ref, lg_ref, lb_ref, *rest, ts, rc, pos0, emit_v):
    if emit_v:
        mix_ref, npool_ref, nconv_ref, vrows_ref, pext_ref, cext_ref, y_ref = rest
    else:
        mix_ref, npool_ref, nconv_ref, pext_ref, cext_ref, y_ref = rest
        vrows_ref = None
    s = pl.program_id(1)
    n_pg = POOL_W // GROUP_W
    n_cg = CONV_W // GROUP_W
    lanes = lambda g: slice(g * GROUP_W, (g + 1) * GROUP_W)

    @pl.when(s == 0)
    def _():
        for g in range(n_pg):
            pext_ref[g, 0:POOL_PAD, :] = ph_ref[:, lanes(g)]
        for g in range(n_cg):
            cext_ref[g, 0:CONV_PAD, :] = ch_ref[:, lanes(g)]

    ii = lax.broadcasted_iota(jnp.int32, (rc, rc), 0)
    jj = lax.broadcasted_iota(jnp.int32, (rc, rc), 1)
    causal = (jj // CAUSAL_CHUNK) <= (ii // CAUSAL_CHUNK)

    def shifted(ref, g, start):
        return ref[g, pl.ds(start, rc, stride=1), :]

    def body(r):
        r0 = r.start
        p_cur = pl.ds(pl.multiple_of(r0 + POOL_PAD, 8), rc)
        c_cur = pl.ds(pl.multiple_of(r0 + CONV_PAD, 8), rc)
        pos = pos0 + s * ts + r0 + lax.broadcasted_iota(jnp.int32, (rc, 1), 0)
        for g, w in enumerate(POOL_WINDOWS):
            xa = z_ref[r, lanes(g)]
            pext_ref[g, p_cur, :] = xa
            acc = xa
            for k in range(1, w):
                acc = acc + shifted(pext_ref, g, r0 + POOL_PAD - k)
            cnt = jnp.minimum(pos + 1, w).astype(F32)
            pooled = acc / cnt - xa
            ya = jnp.dot(pooled.astype(BF16), pw_ref[g].astype(BF16), preferred_element_type=F32)
            mix_ref[r, lanes(g)] = (ya * ps_ref[:, lanes(g)]).astype(BF16)
        u = z_ref[r, POOL_W:POOL_W + SGU_W]
        vn = _rms(z_ref[r, POOL_W + SGU_W:POOL_W + 2 * SGU_W], sn_ref[...])
        if emit_v:
            vrows_ref[r, :] = vn
        vb = vn.astype(BF16)
        for h in range(SGU_HEADS):
            wm = jnp.where(causal, ws_ref[h, 0:rc, 0:rc], 0.0).astype(BF16)
            mixed = jnp.dot(wm, vb[:, lanes(h)], preferred_element_type=F32) + sbt_ref[0:rc, h:h + 1]
            mix_ref[r, POOL_W + h * GROUP_W:POOL_W + (h + 1) * GROUP_W] = (u[:, lanes(h)] * mixed).astype(BF16)
        c0 = POOL_W + 2 * SGU_W
        for g in range(n_cg):
            a = z_ref[r, c0 + g * GROUP_W:c0 + (g + 1) * GROUP_W]
            gate = z_ref[r, c0 + CONV_W + g * GROUP_W:c0 + CONV_W + (g + 1) * GROUP_W]
            hcur = a * _sigmoid(gate)
            cext_ref[g, c_cur, :] = hcur
            acc = cb_ref[:, lanes(g)] + cw_ref[CONV_K - 1:CONV_K, lanes(g)] * hcur
            for k in range(CONV_K - 1):
                acc = acc + cw_ref[k:k + 1, lanes(g)] * shifted(cext_ref, g, r0 + CONV_PAD - CONV_HIST + k)
            y_ref[:, lanes(g)] = acc
        y = y_ref[...]
        mu = jnp.mean(y, axis=-1, keepdims=True)
        yc = y - mu
        yn = yc * lax.rsqrt(jnp.mean(yc * yc, axis=-1, keepdims=True) + LN_EPS) * lg_ref[...] + lb_ref[...]
        mix_ref[r, POOL_W + SGU_W:] = (yn * _sigmoid(yn)).astype(BF16)

    _row_loop(ts, rc, body)

    last = s == pl.num_programs(1) - 1
    for g in range(n_pg):
        tail = pext_ref[g, ts:ts + POOL_PAD, :]
        pext_ref[g, 0:POOL_PAD, :] = tail

        @pl.when(last)
        def _():
            npool_ref[:, lanes(g)] = tail
    for g in range(n_cg):
        tail = cext_ref[g, ts:ts + CONV_PAD, :]
        cext_ref[g, 0:CONV_PAD, :] = tail

        @pl.when(last)
        def _():
            nconv_ref[:, lanes(g)] = tail


def _mixer(z, pool_hist, conv_hist, p, layer, *, batch, seq, ts, pos0, emit_v):
    in_w = z.shape[-1]
    mix_w = POOL_W + SGU_W + CONV_W
    rc = min(ROW_CHUNK, ts)
    n_s = seq // ts
    z3 = z.reshape(batch, seq, in_w)

    def per_layer(shape):
        nd = len(shape)
        return pl.BlockSpec((None,) + shape, lambda b, s: (layer,) + (0,) * nd)

    in_specs = [
        pl.BlockSpec((None, ts, in_w), lambda b, s: (b, s, 0)),
        pl.BlockSpec((None, POOL_PAD, POOL_W), lambda b, s: (b, 0, 0)),
        pl.BlockSpec((None, CONV_PAD, CONV_W), lambda b, s: (b, 0, 0)),
        per_layer((len(POOL_WINDOWS), GROUP_W, GROUP_W)),
        per_layer((1, POOL_W)),
        per_layer((1, SGU_W)),
        per_layer((SGU_HEADS, SGU_CHUNK, SGU_CHUNK)),
        per_layer((SGU_CHUNK, SGU_HEADS)),
        per_layer((CONV_K, CONV_W)),
        per_layer((1, CONV_W)),
        per_layer((1, CONV_W)),
        per_layer((1, CONV_W)),
    ]
    out_specs = [
        pl.BlockSpec((None, ts, mix_w), lambda b, s: (b, s, 0)),
        pl.BlockSpec((None, POOL_PAD, POOL_W), lambda b, s: (b, 0, 0)),
        pl.BlockSpec((None, CONV_PAD, CONV_W), lambda b, s: (b, 0, 0)),
    ]
    out_shape = [
        jax.ShapeDtypeStruct((batch, seq, mix_w), BF16),
        jax.ShapeDtypeStruct((batch, POOL_PAD, POOL_W), F32),
        jax.ShapeDtypeStruct((batch, CONV_PAD, CONV_W), F32),
    ]
    if emit_v:
        out_specs.append(pl.BlockSpec((None, ts, SGU_W), lambda b, s: (b, s, 0)))
        out_shape.append(jax.ShapeDtypeStruct((batch, seq, SGU_W), F32))
    outs = pl.pallas_call(
        functools.partial(_mixer_kernel, ts=ts, rc=rc, pos0=pos0, emit_v=emit_v),
        grid=(batch, n_s),
        in_specs=in_specs,
        out_specs=out_specs,
        out_shape=out_shape,
        scratch_shapes=[
            pltpu.VMEM((POOL_W // GROUP_W, POOL_PAD + ts, GROUP_W), F32),
            pltpu.VMEM((CONV_W // GROUP_W, CONV_PAD + ts, GROUP_W), F32),
            pltpu.VMEM((rc, CONV_W), F32),
        ],
        compiler_params=_params(("arbitrary", "arbitrary")),
        name="mixer",
    )(z3, pool_hist, conv_hist, p["pool_w"], p["pool_scale"], p["sgu_norm"], p["sgu_ws"], p["sgu_bt"],
      p["conv_w"], p["conv_b"], p["conv_ln_g"], p["conv_ln_b"])
    mix = outs[0].reshape(batch * seq, mix_w)
    new_pool = outs[1][:, POOL_PAD - POOL_HIST:, :]
    new_conv = outs[2][:, CONV_PAD - CONV_HIST:, :]
    vrows = outs[3] if emit_v else None
    return mix, new_pool, new_conv, vrows


def _out_proj_kernel(mix_ref, x_ref, g_ref, w_ref, o_ref, y_ref, *, tm, rc):
    y_ref[...] = jnp.dot(mix_ref[...], w_ref[...], preferred_element_type=F32)

    def body(r):
        o_ref[r, :] = x_ref[r, :] + _rms(y_ref[r, :], g_ref[...])
    _row_loop(tm, rc, body)


def _out_proj(mix, x, g, w, layer, *, tm):
    m, d = x.shape
    k = mix.shape[-1]
    rc = min(ROW_CHUNK, tm)
    return pl.pallas_call(
        functools.partial(_out_proj_kernel, tm=tm, rc=rc),
        grid=(m // tm,),
        in_specs=[
            pl.BlockSpec((tm, k), lambda i: (i, 0)),
            pl.BlockSpec((tm, d), lambda i: (i, 0)),
            pl.BlockSpec((None, 1, d), lambda i: (layer, 0, 0)),
            pl.BlockSpec((None, k, d), lambda i: (layer, 0, 0)),
        ],
        out_specs=pl.BlockSpec((tm, d), lambda i: (i, 0)),
        out_shape=jax.ShapeDtypeStruct((m, d), F32),
        scratch_shapes=[pltpu.VMEM((tm, d), F32)],
        compiler_params=_params(("parallel",)),
        name="out_proj",
    )(mix, x, g, w)


def _ffn_kernel(x_ref, gpre_ref, gpost_ref, wg_ref, wu_ref, wd_ref, o_ref, f_ref, acc_ref, *, tm, rc):
    j = pl.program_id(1)

    @pl.when(j == 0)
    def _():
        def body(r):
            f_ref[r, :] = _rms(x_ref[r, :], gpre_ref[...]).astype(BF16)
            acc_ref[r, :] = jnp.zeros((rc, acc_ref.shape[-1]), F32)
        _row_loop(tm, rc, body)

    f = f_ref[...]
    gate = jnp.dot(f, wg_ref[...], preferred_element_type=F32)
    up = jnp.dot(f, wu_ref[...], preferred_element_type=F32)
    act = (gate * _sigmoid(gate) * up).astype(BF16)
    acc_ref[...] += jnp.dot(act, wd_ref[...], preferred_element_type=F32)

    @pl.when(j == pl.num_programs(1) - 1)
    def _():
        def body(r):
            o_ref[r, :] = x_ref[r, :] + _rms(acc_ref[r, :], gpost_ref[...])
        _row_loop(tm, rc, body)


def _ffn(x, gpre, gpost, wg, wu, wd, layer, *, tm, tf):
    m, d = x.shape
    dff = wg.shape[-1]
    rc = min(ROW_CHUNK, tm)
    return pl.pallas_call(
        functools.partial(_ffn_kernel, tm=tm, rc=rc),
        grid=(m // tm, dff // tf),
        in_specs=[
            pl.BlockSpec((tm, d), lambda i, j: (i, 0)),
            pl.BlockSpec((None, 1, d), lambda i, j: (layer, 0, 0)),
            pl.BlockSpec((None, 1, d), lambda i, j: (layer, 0, 0)),
            pl.BlockSpec((None, d, tf), lambda i, j: (layer, 0, j)),
            pl.BlockSpec((None, d, tf), lambda i, j: (layer, 0, j)),
            pl.BlockSpec((None, tf, d), lambda i, j: (layer, j, 0)),
        ],
        out_specs=pl.BlockSpec((tm, d), lambda i, j: (i, 0)),
        out_shape=jax.ShapeDtypeStruct((m, d), F32),
        scratch_shapes=[pltpu.VMEM((tm, d), BF16), pltpu.VMEM((tm, d), F32)],
        compiler_params=_params(("parallel", "arbitrary")),
        name="ffn",
    )(x, gpre, gpost, wg, wu, wd)


def _tile(m, target):
    t = min(m, target)
    assert m % t == 0, (m, t)
    return t


def _trunk_layer(x, pool_hist, conv_hist, p, layer, *, batch, seq, pos0, emit_v):
    m = x.shape[0]
    tm = _tile(m, TM_TARGET)
    z = _in_proj(x, p["norm_pre_mix"], p["w_in"], layer, tm=tm, tn=TN_IN)
    mix, new_pool, new_conv, vrows = _mixer(z, pool_hist, conv_hist, p, layer, batch=batch, seq=seq,
                                            ts=_tile(seq, TS_TARGET), pos0=pos0, emit_v=emit_v)
    x1 = _out_proj(mix, x, p["norm_post_mix"], p["w_out"], layer, tm=tm)
    x2 = _ffn(x1, p["norm_pre_ffn"], p["norm_post_ffn"], p["w_gate"], p["w_up"], p["w_down"], layer,
              tm=tm, tf=TF_FFN)
    return x2, new_pool, new_conv, vrows


def kernel(x_prompt, x_sample, state_pool, state_conv, norm_pre_mix, norm_post_mix, norm_pre_ffn, norm_post_ffn, w_in, w_out, pool_w, pool_scale, sgu_norm, sgu_ws, sgu_b, conv_w, conv_b, conv_ln_g, conv_ln_b, w_gate, w_up, w_down):
    depth = w_in.shape[0]
    bp, sp, d = x_prompt.shape
    bs, ss, _ = x_sample.shape
    row = lambda a: a[:, None, :]
    p = {
        "norm_pre_mix": row(norm_pre_mix), "norm_post_mix": row(norm_post_mix),
        "norm_pre_ffn": row(norm_pre_ffn), "norm_post_ffn": row(norm_post_ffn),
        "w_in": w_in.astype(BF16), "w_out": w_out.astype(BF16),
        "w_gate": w_gate.astype(BF16), "w_up": w_up.astype(BF16), "w_down": w_down.astype(BF16),
        "pool_w": pool_w, "pool_scale": row(pool_scale), "sgu_norm": row(sgu_norm),
        "sgu_ws": sgu_ws, "sgu_bt": jnp.swapaxes(sgu_b, 1, 2),
        "conv_w": conv_w, "conv_b": row(conv_b), "conv_ln_g": row(conv_ln_g), "conv_ln_b": row(conv_ln_b),
    }
    zero_pool = jnp.zeros((bp, POOL_PAD, POOL_W), F32)
    zero_conv = jnp.zeros((bp, CONV_PAD, CONV_W), F32)
    pad_pool = jnp.pad(state_pool, ((0, 0), (0, 0), (POOL_PAD - POOL_HIST, 0), (0, 0)))
    pad_conv = jnp.pad(state_conv, ((0, 0), (0, 0), (CONV_PAD - CONV_HIST, 0), (0, 0)))

    hp = x_prompt.reshape(bp * sp, d)
    hs = x_sample.reshape(bs * ss, d)
    pool_p, conv_p, pool_s, conv_s, vrows_s = [], [], [], [], []
    for l in range(depth):
        hp, npool, nconv, _ = _trunk_layer(hp, zero_pool, zero_conv, p, l, batch=bp, seq=sp, pos0=0,
                                           emit_v=False)
        pool_p.append(npool)
        conv_p.append(nconv)
        hs, npool_s, nconv_s, v_s = _trunk_layer(hs, pad_pool[l], pad_conv[l], p, l, batch=bs, seq=ss,
                                                 pos0=PAST_LEN, emit_v=True)
        pool_s.append(npool_s)
        conv_s.append(nconv_s)
        vrows_s.append(v_s)
    return (hp.reshape(bp, sp, d), hs.reshape(bs, ss, d), jnp.stack(pool_p), jnp.stack(conv_p),
            jnp.stack(pool_s), jnp.stack(conv_s), jnp.stack(vrows_s))
```

```python
import functools

import jax
import jax.numpy as jnp
from jax import lax
from jax.experimental import pallas as pl
from jax.experimental.pallas import tpu as pltpu

F32 = jnp.float32
BF16 = jnp.bfloat16

GROUP_W = 128
POOL_WINDOWS = (2, 4, 8, 16)
POOL_W = len(POOL_WINDOWS) * GROUP_W
POOL_HIST = max(POOL_WINDOWS) - 1
POOL_PAD = 16
SGU_HEADS = 6
SGU_W = SGU_HEADS * GROUP_W
SGU_CHUNK = 128
CAUSAL_CHUNK = 64
CONV_W = 6 * GROUP_W
CONV_K = 31
CONV_HIST = CONV_K - 1
CONV_PAD = 32
MIX_W = POOL_W + SGU_W + CONV_W
PAST_LEN = 4096
RMS_EPS = 1e-6
LN_EPS = 1e-5

ROW_CHUNK = 128
MIX_ROWS = 512
MIX_ROWS_BATCHED = 256
FFN_ROWS = 1024
FFN_COLS = 512
VMEM_LIMIT_BYTES = 58 * 1024 * 1024


def _rms(x, g):
    ms = jnp.mean(x * x, axis=-1, keepdims=True)
    return x * lax.rsqrt(ms + RMS_EPS) * g


def _sigmoid(x):
    return 1.0 / (1.0 + jnp.exp(-x))


def _row_loop(n_rows, chunk, body):
    def step(c, carry):
        body(pl.ds(pl.multiple_of(c * chunk, chunk), chunk))
        return carry
    lax.fori_loop(0, n_rows // chunk, step, 0)


def _params(sem):
    return pltpu.CompilerParams(dimension_semantics=sem, vmem_limit_bytes=VMEM_LIMIT_BYTES)


def _mix_layer_kernel(x_ref, ph_ref, ch_ref, gpre_ref, win_ref, pw_ref, ps_ref, sn_ref, ws_ref, sbt_ref,
                      cw_ref, cb_ref, lg_ref, lb_ref, gpost_ref, wout_ref, *rest,
                      nb, ts, rc, pos0, emit_v):
    if emit_v:
        x1_ref, npool_ref, nconv_ref, vrows_ref, act_ref, z_ref, pext_ref, cext_ref, y_ref = rest
    else:
        x1_ref, npool_ref, nconv_ref, act_ref, z_ref, pext_ref, cext_ref, y_ref = rest
        vrows_ref = None
    rows = nb * ts
    d = x_ref.shape[-1]
    s = pl.program_id(1)
    n_s = pl.num_programs(1)
    n_pg = POOL_W // GROUP_W
    n_cg = CONV_W // GROUP_W
    lanes = lambda g: slice(g * GROUP_W, (g + 1) * GROUP_W)
    norm_rc = min(ROW_CHUNK, rows)

    def norm_in(r):
        act_ref[r, :] = _rms(x_ref[r, :], gpre_ref[...]).astype(BF16)
    _row_loop(rows, norm_rc, norm_in)
    z_ref[...] = jnp.dot(act_ref[...], win_ref[...], preferred_element_type=F32)

    ii = lax.broadcasted_iota(jnp.int32, (rc, rc), 0)
    jj = lax.broadcasted_iota(jnp.int32, (rc, rc), 1)
    causal = (jj // CAUSAL_CHUNK) <= (ii // CAUSAL_CHUNK)

    def shifted(ref, g, start):
        return ref[g, pl.ds(start, rc, stride=1), :]

    def stream(q):
        def load_hist():
            for g in range(n_pg):
                pext_ref[g, 0:POOL_PAD, :] = ph_ref[q, :, lanes(g)]
            for g in range(n_cg):
                cext_ref[g, 0:CONV_PAD, :] = ch_ref[q, :, lanes(g)]
        if nb > 1:
            load_hist()
        else:
            pl.when(s == 0)(load_hist)

        def chunk(rl):
            r0 = rl.start
            r = pl.ds(pl.multiple_of(q * ts + r0, rc), rc)
            p_cur = pl.ds(pl.multiple_of(r0 + POOL_PAD, 8), rc)
            c_cur = pl.ds(pl.multiple_of(r0 + CONV_PAD, 8), rc)
            pos = pos0 + s * ts + r0 + lax.broadcasted_iota(jnp.int32, (rc, 1), 0)
            for g, w in enumerate(POOL_WINDOWS):
                xa = z_ref[r, lanes(g)]
                pext_ref[g, p_cur, :] = xa
                acc = xa
                for k in range(1, w):
                    acc = acc + shifted(pext_ref, g, r0 + POOL_PAD - k)
                cnt = jnp.minimum(pos + 1, w).astype(F32)
                pooled = acc / cnt - xa
                ya = jnp.dot(pooled.astype(BF16), pw_ref[g].astype(BF16), preferred_element_type=F32)
                act_ref[r, lanes(g)] = (ya * ps_ref[:, lanes(g)]).astype(BF16)
            u = z_ref[r, POOL_W:POOL_W + SGU_W]
            vn = _rms(z_ref[r, POOL_W + SGU_W:POOL_W + 2 * SGU_W], sn_ref[...])
            if emit_v:
                vrows_ref[r, :] = vn
            vb = vn.astype(BF16)
            for h in range(SGU_HEADS):
                wm = jnp.where(causal, ws_ref[h, 0:rc, 0:rc], 0.0).astype(BF16)
                mixed = jnp.dot(wm, vb[:, lanes(h)], preferred_element_type=F32) + sbt_ref[0:rc, h:h + 1]
                act_ref[r, POOL_W + h * GROUP_W:POOL_W + (h + 1) * GROUP_W] = (u[:, lanes(h)] * mixed).astype(BF16)
            c0 = POOL_W + 2 * SGU_W
            for g in range(n_cg):
                a = z_ref[r, c0 + g * GROUP_W:c0 + (g + 1) * GROUP_W]
                gate = z_ref[r, c0 + CONV_W + g * GROUP_W:c0 + CONV_W + (g + 1) * GROUP_W]
                hcur = a * _sigmoid(gate)
                cext_ref[g, c_cur, :] = hcur
                acc = cb_ref[:, lanes(g)] + cw_ref[CONV_K - 1:CONV_K, lanes(g)] * hcur
                for k in range(CONV_K - 1):
                    acc = acc + cw_ref[k:k + 1, lanes(g)] * shifted(cext_ref, g, r0 + CONV_PAD - CONV_HIST + k)
                y_ref[:, lanes(g)] = acc
            y = y_ref[...]
            mu = jnp.mean(y, axis=-1, keepdims=True)
            yc = y - mu
            yn = yc * lax.rsqrt(jnp.mean(yc * yc, axis=-1, keepdims=True) + LN_EPS) * lg_ref[...] + lb_ref[...]
            act_ref[r, POOL_W + SGU_W:] = (yn * _sigmoid(yn)).astype(BF16)

        _row_loop(ts, rc, chunk)

        last = s == n_s - 1
        for ext_ref, out_ref, pad, n_g in ((pext_ref, npool_ref, POOL_PAD, n_pg), (cext_ref, nconv_ref, CONV_PAD, n_cg)):
            for g in range(n_g):
                tail = ext_ref[g, ts:ts + pad, :]
                if nb > 1:
                    out_ref[q, :, lanes(g)] = tail
                else:
                    ext_ref[g, 0:pad, :] = tail

                    @pl.when(last)
                    def _():
                        out_ref[q, :, lanes(g)] = tail

    if nb > 1:
        def stream_step(q, carry):
            stream(q)
            return carry
        lax.fori_loop(0, nb, stream_step, 0)
    else:
        stream(0)

    z_ref[:, 0:d] = jnp.dot(act_ref[...], wout_ref[...], preferred_element_type=F32)

    def resid(r):
        x1_ref[r, :] = x_ref[r, :] + _rms(z_ref[r, 0:d], gpost_ref[...])
    _row_loop(rows, norm_rc, resid)


def _mix_layer(x, pool_hist, conv_hist, p, layer, *, batch, seq, pos0, emit_v):
    m, d = x.shape
    in_w = p["w_in"].shape[-1]
    assert p["w_out"].shape[-2] == MIX_W and MIX_W == d
    if seq >= MIX_ROWS:
        nb, ts = 1, MIX_ROWS
    else:
        nb, ts = min(batch, MIX_ROWS_BATCHED // seq), seq
    assert seq % ts == 0 and batch % nb == 0
    n_s = seq // ts
    rows = nb * ts
    rc = min(ROW_CHUNK, ts)

    def per_layer(shape, **kw):
        nd = len(shape)
        return pl.BlockSpec((None,) + shape, lambda b, s: (layer,) + (0,) * nd, **kw)

    resident = dict(pipeline_mode=pl.Buffered(1))
    in_specs = [
        pl.BlockSpec((rows, d), lambda b, s: (b * n_s + s, 0)),
        pl.BlockSpec((nb, POOL_PAD, POOL_W), lambda b, s: (b, 0, 0)),
        pl.BlockSpec((nb, CONV_PAD, CONV_W), lambda b, s: (b, 0, 0)),
        per_layer((1, d)),
        per_layer((d, in_w), **resident),
        per_layer((len(POOL_WINDOWS), GROUP_W, GROUP_W)),
        per_layer((1, POOL_W)),
        per_layer((1, SGU_W)),
        per_layer((SGU_HEADS, SGU_CHUNK, SGU_CHUNK)),
        per_layer((SGU_CHUNK, SGU_HEADS)),
        per_layer((CONV_K, CONV_W)),
        per_layer((1, CONV_W)),
        per_layer((1, CONV_W)),
        per_layer((1, CONV_W)),
        per_layer((1, d)),
        per_layer((MIX_W, d), **resident),
    ]
    out_specs = [
        pl.BlockSpec((rows, d), lambda b, s: (b * n_s + s, 0)),
        pl.BlockSpec((nb, POOL_PAD, POOL_W), lambda b, s: (b, 0, 0)),
        pl.BlockSpec((nb, CONV_PAD, CONV_W), lambda b, s: (b, 0, 0)),
    ]
    out_shape = [
        jax.ShapeDtypeStruct((m, d), F32),
        jax.ShapeDtypeStruct((batch, POOL_PAD, POOL_W), F32),
        jax.ShapeDtypeStruct((batch, CONV_PAD, CONV_W), F32),
    ]
    if emit_v:
        out_specs.append(pl.BlockSpec((rows, SGU_W), lambda b, s: (b * n_s + s, 0)))
        out_shape.append(jax.ShapeDtypeStruct((m, SGU_W), F32))
    outs = pl.pallas_call(
        functools.partial(_mix_layer_kernel, nb=nb, ts=ts, rc=rc, pos0=pos0, emit_v=emit_v),
        grid=(batch // nb, n_s),
        in_specs=in_specs,
        out_specs=out_specs,
        out_shape=out_shape,
        scratch_shapes=[
            pltpu.VMEM((rows, d), BF16),
            pltpu.VMEM((rows, in_w), F32),
            pltpu.VMEM((POOL_W // GROUP_W, POOL_PAD + ts, GROUP_W), F32),
            pltpu.VMEM((CONV_W // GROUP_W, CONV_PAD + ts, GROUP_W), F32),
            pltpu.VMEM((rc, CONV_W), F32),
        ],
        compiler_params=_params(("arbitrary", "arbitrary")),
        name="mix_layer",
    )(x, pool_hist, conv_hist, p["norm_pre_mix"], p["w_in"], p["pool_w"], p["pool_scale"], p["sgu_norm"],
      p["sgu_ws"], p["sgu_bt"], p["conv_w"], p["conv_b"], p["conv_ln_g"], p["conv_ln_b"],
      p["norm_post_mix"], p["w_out"])
    x1 = outs[0]
    new_pool = outs[1][:, POOL_PAD - POOL_HIST:, :]
    new_conv = outs[2][:, CONV_PAD - CONV_HIST:, :]
    vrows = outs[3].reshape(batch, seq, SGU_W) if emit_v else None
    return x1, new_pool, new_conv, vrows


def _ffn_kernel(x_ref, gpre_ref, gpost_ref, wg_ref, wu_ref, wd_ref, o_ref, f_ref, *, tm, rc, tn):
    j = pl.program_id(1)
    d = o_ref.shape[-1]

    @pl.when(j == 0)
    def _():
        def body(r):
            f_ref[r, :] = _rms(x_ref[r, :], gpre_ref[...]).astype(BF16)
            o_ref[r, :] = jnp.zeros((rc, d), F32)
        _row_loop(tm, rc, body)

    f = f_ref[...]
    gate = jnp.dot(f, wg_ref[...], preferred_element_type=F32)
    up = jnp.dot(f, wu_ref[...], preferred_element_type=F32)
    act = (gate * _sigmoid(gate) * up).astype(BF16)
    for c in range(d // tn):
        cols = slice(c * tn, (c + 1) * tn)
        o_ref[:, cols] += jnp.dot(act, wd_ref[:, cols], preferred_element_type=F32)

    @pl.when(j == pl.num_programs(1) - 1)
    def _():
        def body(r):
            o_ref[r, :] = x_ref[r, :] + _rms(o_ref[r, :], gpost_ref[...])
        _row_loop(tm, rc, body)


def _ffn(x, p, layer, *, tm, tf):
    m, d = x.shape
    dff = p["w_gate"].shape[-1]
    rc = min(ROW_CHUNK, tm)
    return pl.pallas_call(
        functools.partial(_ffn_kernel, tm=tm, rc=rc, tn=min(d, 512)),
        grid=(m // tm, dff // tf),
        in_specs=[
            pl.BlockSpec((tm, d), lambda i, j: (i, 0), pipeline_mode=pl.Buffered(1)),
            pl.BlockSpec((None, 1, d), lambda i, j: (layer, 0, 0)),
            pl.BlockSpec((None, 1, d), lambda i, j: (layer, 0, 0)),
            pl.BlockSpec((None, d, tf), lambda i, j: (layer, 0, j)),
            pl.BlockSpec((None, d, tf), lambda i, j: (layer, 0, j)),
            pl.BlockSpec((None, tf, d), lambda i, j: (layer, j, 0)),
        ],
        out_specs=pl.BlockSpec((tm, d), lambda i, j: (i, 0)),
        out_shape=jax.ShapeDtypeStruct((m, d), F32),
        scratch_shapes=[pltpu.VMEM((tm, d), BF16)],
        compiler_params=_params(("parallel", "arbitrary")),
        name="ffn",
    )(x, p["norm_pre_ffn"], p["norm_post_ffn"], p["w_gate"], p["w_up"], p["w_down"])


def _tile(m, target):
    t = min(m, target)
    assert m % t == 0, (m, t)
    return t


def _trunk_layer(x, pool_hist, conv_hist, p, layer, *, batch, seq, pos0, emit_v):
    x1, new_pool, new_conv, vrows = _mix_layer(x, pool_hist, conv_hist, p, layer, batch=batch, seq=seq,
                                               pos0=pos0, emit_v=emit_v)
    x2 = _ffn(x1, p, layer, tm=_tile(x.shape[0], FFN_ROWS), tf=FFN_COLS)
    return x2, new_pool, new_conv, vrows


def kernel(x_prompt, x_sample, state_pool, state_conv, norm_pre_mix, norm_post_mix, norm_pre_ffn, norm_post_ffn, w_in, w_out, pool_w, pool_scale, sgu_norm, sgu_ws, sgu_b, conv_w, conv_b, conv_ln_g, conv_ln_b, w_gate, w_up, w_down):
    depth = w_in.shape[0]
    bp, sp, d = x_prompt.shape
    bs, ss, _ = x_sample.shape
    row = lambda a: a[:, None, :]
    p = {
        "norm_pre_mix": row(norm_pre_mix), "norm_post_mix": row(norm_post_mix),
        "norm_pre_ffn": row(norm_pre_ffn), "norm_post_ffn": row(norm_post_ffn),
        "w_in": w_in.astype(BF16), "w_out": w_out.astype(BF16),
        "w_gate": w_gate.astype(BF16), "w_up": w_up.astype(BF16), "w_down": w_down.astype(BF16),
        "pool_w": pool_w, "pool_scale": row(pool_scale), "sgu_norm": row(sgu_norm),
        "sgu_ws": sgu_ws, "sgu_bt": jnp.swapaxes(sgu_b, 1, 2),
        "conv_w": conv_w, "conv_b": row(conv_b), "conv_ln_g": row(conv_ln_g), "conv_ln_b": row(conv_ln_b),
    }
    zero_pool = jnp.zeros((bp, POOL_PAD, POOL_W), F32)
    zero_conv = jnp.zeros((bp, CONV_PAD, CONV_W), F32)
    pad_pool = jnp.pad(state_pool, ((0, 0), (0, 0), (POOL_PAD - POOL_HIST, 0), (0, 0)))
    pad_conv = jnp.pad(state_conv, ((0, 0), (0, 0), (CONV_PAD - CONV_HIST, 0), (0, 0)))

    hp = x_prompt.reshape(bp * sp, d)
    hs = x_sample.reshape(bs * ss, d)
    pool_p, conv_p, pool_s, conv_s, vrows_s = [], [], [], [], []
    for l in range(depth):
        hp, npool, nconv, _ = _trunk_layer(hp, zero_pool, zero_conv, p, l, batch=bp, seq=sp, pos0=0,
                                           emit_v=False)
        pool_p.append(npool)
        conv_p.append(nconv)
        hs, npool_s, nconv_s, v_s = _trunk_layer(hs, pad_pool[l], pad_conv[l], p, l, batch=bs, seq=ss,
                                                 pos0=PAST_LEN, emit_v=True)
        pool_s.append(npool_s)
        conv_s.append(nconv_s)
        vrows_s.append(v_s)
    return (hp.reshape(bp, sp, d), hs.reshape(bs, ss, d), jnp.stack(pool_p), jnp.stack(conv_p),
            jnp.stack(pool_s), jnp.stack(conv_s), jnp.stack(vrows_s))
```
